```python
import math
import jax, jax.numpy as jnp
from jax import lax
import numpy as np

D_MODEL = 1024
BATCH = 16
SEQ = 2048
DEPTH = 4

CHUNK = 64
N_META = 16
CONV_W = 3
N_HEADS = 8
HEAD_DIM = D_MODEL // N_HEADS // 2
V_DIM = 2 * HEAD_DIM
Q_BLOCK = 128
N_A = max(1, DEPTH // 2)
N_B = DEPTH - N_A
EPS = 1e-6

kernel_name = "yoco_shortconv_diffattn_meta"


def rmsnorm(x, g):
    xf = x.astype(jnp.float32)
    y = xf * lax.rsqrt(jnp.mean(xf * xf, axis=-1, keepdims=True) + EPS)
    return (y * g.astype(jnp.float32)).astype(x.dtype)


def chunk_ids(n):
    p = jnp.arange(n)
    return jnp.where(p < N_META, 0, 1 + (p - N_META) // CHUNK)


def lambda_init_fn(depth):
    return 0.8 - 0.6 * math.exp(-0.3 * depth)


def short_conv_mixer(h, g, w_in, conv_w, conv_b, w_out):
    L = h.shape[1]
    u = rmsnorm(h, g) @ w_in
    b_gate, c_gate, h_in, z = jnp.split(u, 4, axis=-1)
    v = c_gate * h_in
    vp = jnp.pad(v, ((0, 0), (CONV_W - 1, 0), (0, 0)))
    conv = conv_w[0] * vp[:, 0:L]
    for j in range(1, CONV_W):
        conv = conv + conv_w[j] * vp[:, j:j + L]
    conv = conv + conv_b
    y = b_gate * conv * jax.nn.silu(z)
    return y @ w_out


def diff_attention_core(q, k, v, lam):
    bsz, L = q.shape[0], q.shape[1]
    n_blocks = -(-L // Q_BLOCK)
    Lp = n_blocks * Q_BLOCK
    qp = jnp.pad(q, ((0, 0), (0, Lp - L), (0, 0), (0, 0), (0, 0)))
    qb = qp.reshape(bsz, n_blocks, Q_BLOCK, N_HEADS, 2, HEAD_DIM).transpose(1, 0, 2, 3, 4, 5)
    qcid = chunk_ids(Lp).reshape(n_blocks, Q_BLOCK)
    kcid = chunk_ids(L)
    kf = k.astype(jnp.float32)
    vf = v.astype(jnp.float32)
    scale = HEAD_DIM ** -0.5

    def block(args):
        qi, ci = args
        s = jnp.einsum('bqhcd,bkhcd->bhcqk', qi.astype(jnp.float32), kf) * scale
        mask = ci[:, None] >= kcid[None, :]
        s = jnp.where(mask[None, None, None], s, -jnp.inf)
        p = jax.nn.softmax(s, axis=-1)
        attn = p[:, :, 0] - lam * p[:, :, 1]
        return jnp.einsum('bhqk,bkhe->bqhe', attn, vf)

    o = lax.map(block, (qb, qcid))
    o = o.transpose(1, 0, 2, 3, 4).reshape(bsz, Lp, N_HEADS, V_DIM)
    return o[:, :L]


def diff_attn_mixer(h, g, w_in, lq1, lk1, lq2, lk2, subln_g, w_out, k, v, lambda_init):
    bsz, L, _ = h.shape
    u = rmsnorm(h, g) @ w_in
    q, z = jnp.split(u, 2, axis=-1)
    q = q.reshape(bsz, L, N_HEADS, 2, HEAD_DIM)
    lam = (jnp.exp(jnp.sum(lq1.astype(jnp.float32) * lk1.astype(jnp.float32)))
           - jnp.exp(jnp.sum(lq2.astype(jnp.float32) * lk2.astype(jnp.float32)))
           + lambda_init)
    o = diff_attention_core(q, k, v, lam)
    o = rmsnorm(o, subln_g) * (1.0 - lambda_init)
    o = o.reshape(bsz, L, N_HEADS * V_DIM).astype(h.dtype) * jax.nn.silu(z)
    return o @ w_out


def setup_inputs(seed: int = 0) -> dict:
    key = jax.random.key(seed)
    ks = jax.random.split(key, 20)
    D = D_MODEL
    s = D ** -0.5
    nrm = lambda k, shp, sc: jax.random.normal(k, shp, jnp.float32) * sc
    return {
        "x": nrm(ks[0], (BATCH, SEQ, D), 1.0),
        "meta_tokens": nrm(ks[1], (N_META, D), 1.0),
        "a_norm_g": 1.0 + nrm(ks[2], (N_A, D), 0.02),
        "a_w_in": nrm(ks[3], (N_A, D, 4 * D), s),
        "a_conv_w": nrm(ks[4], (N_A, CONV_W, D), CONV_W ** -0.5),
        "a_conv_b": nrm(ks[5], (N_A, D), 0.02),
        "a_w_out": nrm(ks[6], (N_A, D, D), s),
        "kv_norm_g": 1.0 + nrm(ks[7], (D,), 0.02),
        "w_kv": nrm(ks[8], (D, 2 * D), s),
        "b_norm_g": 1.0 + nrm(ks[9], (N_B, D), 0.02),
        "b_w_in": nrm(ks[10], (N_B, D, 2 * D), s),
        "b_lambda_q1": nrm(ks[11], (N_B, HEAD_DIM), 0.1),
        "b_lambda_k1": nrm(ks[12], (N_B, HEAD_DIM), 0.1),
        "b_lambda_q2": nrm(ks[13], (N_B, HEAD_DIM), 0.1),
        "b_lambda_k2": nrm(ks[14], (N_B, HEAD_DIM), 0.1),
        "b_subln_g": 1.0 + nrm(ks[15], (N_B, V_DIM), 0.02),
        "b_w_out": nrm(ks[16], (N_B, D, D), s),
        "final_norm_g": 1.0 + nrm(ks[17], (D,), 0.02),
    }


def reference(x, meta_tokens, a_norm_g, a_w_in, a_conv_w, a_conv_b, a_w_out,
              kv_norm_g, w_kv, b_norm_g, b_w_in, b_lambda_q1, b_lambda_k1,
              b_lambda_q2, b_lambda_k2, b_subln_g, b_w_out, final_norm_g):
    bsz = x.shape[0]
    meta = jnp.broadcast_to(meta_tokens[None].astype(x.dtype), (bsz, N_META, D_MODEL))
    h = jnp.concatenate([meta, x], axis=1)
    L = h.shape[1]
    k_sh = None
    v_sh = None
    for i in range(DEPTH):
        if i < N_A:
            h = h + short_conv_mixer(h, a_norm_g[i], a_w_in[i], a_conv_w[i],
                                     a_conv_b[i], a_w_out[i])
            if i == N_A - 1:
                kv = rmsnorm(h, kv_norm_g) @ w_kv
                k_sh = kv[..., :D_MODEL].reshape(bsz, L, N_HEADS, 2, HEAD_DIM)
                v_sh = kv[..., D_MODEL:].reshape(bsz, L, N_HEADS, V_DIM)
        else:
            j = i - N_A
            h = h + diff_attn_mixer(h, b_norm_g[j], b_w_in[j], b_lambda_q1[j],
                                    b_lambda_k1[j], b_lambda_q2[j], b_lambda_k2[j],
                                    b_subln_g[j], b_w_out[j], k_sh, v_sh,
                                    lambda_init_fn(i))
    h = rmsnorm(h, final_norm_g)
    return h[:, N_META:]
```

```python
import functools
import math

import jax
import jax.numpy as jnp
from jax import lax
from jax.experimental import pallas as pl
from jax.experimental.pallas import tpu as pltpu

D_MODEL = 1024
N_HEADS = 8
HEAD_DIM = 64
V_DIM = 128
N_META = 16
CHUNK = 64
CONV_W = 3
N_A = 2
EPS = 1e-6

F32 = jnp.float32
BF16 = jnp.bfloat16

V7X_VMEM_BYTES = 64 * 1024 * 1024
VMEM_LIMIT_BYTES = V7X_VMEM_BYTES * 7 // 8
SUBLANES = 8
BF16_ROWS = 16

ROW_TILE = 512
CONV_COLS = 256
Q_TILE = 256
K_TILE = 256
Q_SCALE = HEAD_DIM ** -0.5 * math.log2(math.e)


def _lambda_init(depth):
    return 0.8 - 0.6 * math.exp(-0.3 * depth)


def _rmsnorm_rows(x, g):
    ms = jnp.mean(x * x, axis=-1, keepdims=True)
    return x * lax.rsqrt(ms + EPS) * g


def _silu(z):
    return z / (1.0 + jnp.exp(-z))


def _dot(a, b):
    return jnp.dot(a, b, preferred_element_type=F32)


def _params(n_grid):
    return pltpu.CompilerParams(
        dimension_semantics=("arbitrary",) * n_grid,
        vmem_limit_bytes=VMEM_LIMIT_BYTES,
    )


def _const_spec(shape):
    return pl.BlockSpec(shape, lambda *_: (0,) * len(shape))


def _conv_layer_kernel(h_ref, halo_ref, g_ref, win_ref, cw_ref, cb_ref, wout_ref,
                       out_ref, vtail_ref, hn_sc, y_sc, carry_sc, *, tm):
    @pl.when(pl.program_id(1) == 0)
    def _():
        carry_sc[...] = halo_ref[...]

    x = h_ref[...]
    hn_sc[...] = _rmsnorm_rows(x, g_ref[...]).astype(BF16)
    fix = min(tm, BF16_ROWS)
    for j in range(D_MODEL // CONV_COLS):
        lo = j * CONV_COLS
        cols = slice(lo, lo + CONV_COLS)
        hn = hn_sc[...]

        def proj(part):
            return _dot(hn, win_ref[:, part * D_MODEL + lo:part * D_MODEL + lo + CONV_COLS])

        v = proj(1) * proj(2)
        prev = carry_sc[:, cols]
        carry_sc[:, cols] = v[tm - SUBLANES:, :]
        w0 = cw_ref[0:1, cols]
        w1 = cw_ref[1:2, cols]
        w2 = cw_ref[2:3, cols]
        cb = cb_ref[:, cols]
        conv = w0 * pltpu.roll(v, 2, axis=0) + w1 * pltpu.roll(v, 1, axis=0) + w2 * v + cb
        z = proj(3)
        gate = proj(0) * _silu(z)
        y_sc[:, cols] = (gate * conv).astype(BF16)
        ext = jnp.concatenate([prev, v[:fix]], axis=0)
        e1 = pltpu.roll(ext, 1, axis=0)[SUBLANES:]
        e2 = pltpu.roll(ext, 2, axis=0)[SUBLANES:]
        conv_head = w0 * e2 + w1 * e1 + w2 * v[:fix] + cb
        y_sc[0:fix, cols] = (gate[:fix] * conv_head).astype(BF16)
    vtail_ref[...] = carry_sc[...]
    out_ref[...] = x + _dot(y_sc[...], wout_ref[...])


def _conv_layer(h, halo, g, w_in, conv_w, conv_b, w_out):
    bsz, seq, _ = h.shape
    tm = min(seq, ROW_TILE)
    kern = functools.partial(_conv_layer_kernel, tm=tm)
    return pl.pallas_call(
        kern,
        out_shape=(jax.ShapeDtypeStruct(h.shape, F32),
                   jax.ShapeDtypeStruct((bsz, SUBLANES, D_MODEL), F32)),
        grid=(bsz, seq // tm),
        in_specs=[
            pl.BlockSpec((None, tm, D_MODEL), lambda b, i: (b, i, 0)),
            _const_spec((SUBLANES, D_MODEL)),
            _const_spec((1, D_MODEL)),
            _const_spec((D_MODEL, 4 * D_MODEL)),
            _const_spec((CONV_W, D_MODEL)),
            _const_spec((1, D_MODEL)),
            _const_spec((D_MODEL, D_MODEL)),
        ],
        out_specs=(pl.BlockSpec((None, tm, D_MODEL), lambda b, i: (b, i, 0)),
                   pl.BlockSpec((None, SUBLANES, D_MODEL), lambda b, i: (b, 0, 0))),
        scratch_shapes=[pltpu.VMEM((tm, D_MODEL), BF16),
                        pltpu.VMEM((tm, D_MODEL), BF16),
                        pltpu.VMEM((SUBLANES, D_MODEL), F32)],
        compiler_params=_params(2),
        name="conv_layer",
    )(h, halo, g, w_in, conv_w, conv_b, w_out)


def _norm_proj_kernel(h_ref, g_ref, w_ref, a_ref, b_ref, *, a_scale):
    hn = _rmsnorm_rows(h_ref[...], g_ref[...]).astype(BF16)
    a = _dot(hn, w_ref[:, :D_MODEL])
    if a_scale != 1.0:
        a = a * a_scale
    a_ref[...] = a.astype(BF16)
    b_ref[...] = _dot(hn, w_ref[:, D_MODEL:]).astype(BF16)


def _norm_proj(h, g, w, a_scale=1.0):
    bsz, seq, _ = h.shape
    tm = min(seq, ROW_TILE)
    tile = pl.BlockSpec((None, tm, D_MODEL), lambda b, i: (b, i, 0))
    out = jax.ShapeDtypeStruct(h.shape, BF16)
    return pl.pallas_call(
        functools.partial(_norm_proj_kernel, a_scale=a_scale),
        out_shape=(out, out),
        grid=(bsz, seq // tm),
        in_specs=[tile, _const_spec((1, D_MODEL)), _const_spec((D_MODEL, 2 * D_MODEL))],
        out_specs=(tile, tile),
        compiler_params=_params(2),
        name="norm_proj",
    )(h, g, w)


def _out_proj_kernel(h_ref, o_ref, z_ref, w_ref, g_ref, out_ref, *, final_norm):
    y = (o_ref[...].astype(F32) * _silu(z_ref[...].astype(F32))).astype(BF16)
    out = h_ref[...] + _dot(y, w_ref[...])
    if final_norm:
        out = _rmsnorm_rows(out, g_ref[...])
    out_ref[...] = out


def _out_proj(h, o, z, w, g, final_norm):
    bsz, seq, _ = h.shape
    tm = min(seq, ROW_TILE)
    tile = pl.BlockSpec((None, tm, D_MODEL), lambda b, i: (b, i, 0))
    return pl.pallas_call(
        functools.partial(_out_proj_kernel, final_norm=final_norm),
        out_shape=jax.ShapeDtypeStruct(h.shape, F32),
        grid=(bsz, seq // tm),
        in_specs=[tile, tile, tile, _const_spec((D_MODEL, D_MODEL)), _const_spec((1, D_MODEL))],
        out_specs=tile,
        compiler_params=_params(2),
        name="out_proj",
    )(h, o, z, w, g)


def _attn_kernel(qT_ref, k_ref, vT_ref, km_ref, vmT_ref, lq1_ref, lk1_ref, lq2_ref, lk2_ref, g_ref,
                 o_ref, *, seq, lambda_init):
    tq, tk = Q_TILE, K_TILE
    lam = (jnp.exp(jnp.sum(lq1_ref[...] * lk1_ref[...], keepdims=True))
           - jnp.exp(jnp.sum(lq2_ref[...] * lk2_ref[...], keepdims=True)) + lambda_init)
    first_map = lax.broadcasted_iota(jnp.int32, (2 * HEAD_DIM, tq), 0) < HEAD_DIM
    kchunk = lax.broadcasted_iota(jnp.int32, (tk, tq), 0) // CHUNK
    qchunk = lax.broadcasted_iota(jnp.int32, (tk, tq), 1) // CHUNK
    diag_mask = kchunk <= qchunk
    km = km_ref[...]
    vm_aug = jnp.concatenate([vmT_ref[...], jnp.ones((BF16_ROWS, N_META), BF16)], axis=0)
    ones_rows = jnp.ones((BF16_ROWS, tk), BF16)

    def step(kblk, vaug, m, acc, qc, mask):
        s = _dot(kblk, qc)
        if mask is not None:
            s = jnp.where(mask, s, -jnp.inf)
        m_new = jnp.maximum(m, jnp.max(s, axis=0, keepdims=True))
        p = jnp.exp2(s - m_new).astype(BF16)
        return m_new, jnp.exp2(m - m_new) * acc + _dot(vaug, p)

    for i in range(seq // tq):
        qT = qT_ref[:, i * tq:(i + 1) * tq]
        zero = jnp.zeros_like(qT)
        qcs = (jnp.where(first_map, qT, zero), jnp.where(first_map, zero, qT))

        state = []
        for qc in qcs:
            s = _dot(km, qc)
            m = jnp.max(s, axis=0, keepdims=True)
            state += [m, _dot(vm_aug, jnp.exp2(s - m).astype(BF16))]

        def body(j, carry, qcs=qcs):
            kblk = k_ref[pl.ds(pl.multiple_of(j * tk, tk), tk), :]
            vaug = jnp.concatenate([vT_ref[j], ones_rows], axis=0)
            m1, a1 = step(kblk, vaug, carry[0], carry[1], qcs[0], None)
            m2, a2 = step(kblk, vaug, carry[2], carry[3], qcs[1], None)
            return m1, a1, m2, a2

        state = lax.fori_loop(0, i, body, tuple(state))
        kblk = k_ref[i * tk:(i + 1) * tk, :]
        vaug = jnp.concatenate([vT_ref[i], ones_rows], axis=0)
        _, a1 = step(kblk, vaug, state[0], state[1], qcs[0], diag_mask)
        _, a2 = step(kblk, vaug, state[2], state[3], qcs[1], diag_mask)

        o1 = a1[:V_DIM] * (1.0 / a1[V_DIM:V_DIM + 1])
        o2 = a2[:V_DIM] * (1.0 / a2[V_DIM:V_DIM + 1])
        oT = o1 - lam * o2
        ms = jnp.mean(oT * oT, axis=0, keepdims=True)
        oT = oT * lax.rsqrt(ms + EPS) * g_ref[...] * (1.0 - lambda_init)
        o_ref[i * tq:(i + 1) * tq, :] = oT.T.astype(BF16)


def _attention(qT, k, vT, km, vmT, lq1, lk1, lq2, lk2, g_col, lambda_init):
    bsz, seq, _ = k.shape
    assert Q_TILE == K_TILE and seq % Q_TILE == 0 and Q_TILE % CHUNK == 0
    lam_spec = _const_spec((1, HEAD_DIM))
    return pl.pallas_call(
        functools.partial(_attn_kernel, seq=seq, lambda_init=lambda_init),
        out_shape=jax.ShapeDtypeStruct((bsz, seq, D_MODEL), BF16),
        grid=(bsz, N_HEADS),
        in_specs=[
            pl.BlockSpec((None, None, 2 * HEAD_DIM, seq), lambda b, h: (b, h, 0, 0)),
            pl.BlockSpec((None, seq, 2 * HEAD_DIM), lambda b, h: (b, 0, h)),
            pl.BlockSpec((None, None, seq // K_TILE, V_DIM, K_TILE), lambda b, h: (b, h, 0, 0, 0)),
            pl.BlockSpec((N_META, 2 * HEAD_DIM), lambda b, h: (0, h)),
            pl.BlockSpec((None, V_DIM, N_META), lambda b, h: (h, 0, 0)),
            lam_spec, lam_spec, lam_spec, lam_spec,
            _const_spec((V_DIM, 1)),
        ],
        out_specs=pl.BlockSpec((None, seq, V_DIM), lambda b, h: (b, 0, h)),
        compiler_params=_params(2),
        name="diff_attention",
    )(qT, k, vT, km, vmT, lq1, lk1, lq2, lk2, g_col)


def kernel(x, meta_tokens, a_norm_g, a_w_in, a_conv_w, a_conv_b, a_w_out, kv_norm_g, w_kv, b_norm_g,
           b_w_in, b_lambda_q1, b_lambda_k1, b_lambda_q2, b_lambda_k2, b_subln_g, b_w_out,
           final_norm_g):
    bsz, seq, _ = x.shape
    n_b = b_w_in.shape[0]
    row = lambda a: a.reshape(1, -1)

    h = x
    hm = meta_tokens[None]
    halo_m = jnp.zeros((SUBLANES, D_MODEL), F32)
    for i in range(N_A):
        args = (row(a_norm_g[i]), a_w_in[i].astype(BF16), a_conv_w[i], row(a_conv_b[i]),
                a_w_out[i].astype(BF16))
        hm, vtail_m = _conv_layer(hm, halo_m, *args)
        h, _ = _conv_layer(h, vtail_m[0], *args)

    w_kv_b = w_kv.astype(BF16)
    k, v = _norm_proj(h, row(kv_norm_g), w_kv_b)
    km, vm = _norm_proj(hm, row(kv_norm_g), w_kv_b)
    vT = v.reshape(bsz, seq // K_TILE, K_TILE, N_HEADS, V_DIM).transpose(0, 3, 1, 4, 2)
    vmT = vm.reshape(N_META, N_HEADS, V_DIM).transpose(1, 2, 0)
    km = km[0]

    for j in range(n_b):
        lambda_init = _lambda_init(N_A + j)
        q, z = _norm_proj(h, row(b_norm_g[j]), b_w_in[j].astype(BF16), a_scale=Q_SCALE)
        qT = q.reshape(bsz, seq, N_HEADS, 2 * HEAD_DIM).transpose(0, 2, 3, 1)
        o = _attention(qT, k, vT, km, vmT, row(b_lambda_q1[j]), row(b_lambda_k1[j]),
                       row(b_lambda_q2[j]), row(b_lambda_k2[j]), b_subln_g[j].reshape(V_DIM, 1),
                       lambda_init)
        h = _out_proj(h, o, z, b_w_out[j].astype(BF16), row(final_norm_g),
                      final_norm=(j == n_b - 1))
    return h
```

```python
import functools
import math

import jax
import jax.numpy as jnp
from jax import lax
from jax.experimental import pallas as pl
from jax.experimental.pallas import tpu as pltpu

D_MODEL = 1024
N_HEADS = 8
HEAD_DIM = 64
V_DIM = 128
N_META = 16
CHUNK = 64
CONV_W = 3
N_A = 2
EPS = 1e-6

F32 = jnp.float32
BF16 = jnp.bfloat16

V7X_VMEM_BYTES = 64 * 1024 * 1024
VMEM_LIMIT_BYTES = V7X_VMEM_BYTES * 7 // 8
SUBLANES = 8
BF16_ROWS = 16

ROW_TILE = 512
CONV_COLS = 256
Q_TILE = 256
Q_SCALE = HEAD_DIM ** -0.5 * math.log2(math.e)


def _lambda_init(depth):
    return 0.8 - 0.6 * math.exp(-0.3 * depth)


def _rmsnorm_rows(x, g):
    ms = jnp.mean(x * x, axis=-1, keepdims=True)
    return x * lax.rsqrt(ms + EPS) * g


def _silu(z):
    return z / (1.0 + jnp.exp(-z))


def _dot(a, b):
    return jnp.dot(a, b, preferred_element_type=F32)


def _params(n_grid):
    return pltpu.CompilerParams(
        dimension_semantics=("arbitrary",) * n_grid,
        vmem_limit_bytes=VMEM_LIMIT_BYTES,
    )


def _const_spec(shape):
    return pl.BlockSpec(shape, lambda *_: (0,) * len(shape))


def _conv_layer_kernel(h_ref, halo_ref, g_ref, win_ref, cw_ref, cb_ref, wout_ref,
                       out_ref, vtail_ref, hn_sc, y_sc, carry_sc, *, tm):
    @pl.when(pl.program_id(1) == 0)
    def _():
        carry_sc[...] = halo_ref[...]

    x = h_ref[...]
    hn_sc[...] = _rmsnorm_rows(x, g_ref[...]).astype(BF16)
    fix = min(tm, BF16_ROWS)
    for j in range(D_MODEL // CONV_COLS):
        lo = j * CONV_COLS
        cols = slice(lo, lo + CONV_COLS)
        hn = hn_sc[...]

        def proj(part):
            return _dot(hn, win_ref[:, part * D_MODEL + lo:part * D_MODEL + lo + CONV_COLS])

        v = proj(1) * proj(2)
        prev = carry_sc[:, cols]
        carry_sc[:, cols] = v[tm - SUBLANES:, :]
        w0 = cw_ref[0:1, cols]
        w1 = cw_ref[1:2, cols]
        w2 = cw_ref[2:3, cols]
        cb = cb_ref[:, cols]
        conv = w0 * pltpu.roll(v, 2, axis=0) + w1 * pltpu.roll(v, 1, axis=0) + w2 * v + cb
        z = proj(3)
        gate = proj(0) * _silu(z)
        y_sc[:, cols] = (gate * conv).astype(BF16)
        ext = jnp.concatenate([prev, v[:fix]], axis=0)
        e1 = pltpu.roll(ext, 1, axis=0)[SUBLANES:]
        e2 = pltpu.roll(ext, 2, axis=0)[SUBLANES:]
        conv_head = w0 * e2 + w1 * e1 + w2 * v[:fix] + cb
        y_sc[0:fix, cols] = (gate[:fix] * conv_head).astype(BF16)
    vtail_ref[...] = carry_sc[...]
    out_ref[...] = x + _dot(y_sc[...], wout_ref[...])


def _conv_layer(h, halo, g, w_in, conv_w, conv_b, w_out):
    bsz, seq, _ = h.shape
    tm = min(seq, ROW_TILE)
    kern = functools.partial(_conv_layer_kernel, tm=tm)
    return pl.pallas_call(
        kern,
        out_shape=(jax.ShapeDtypeStruct(h.shape, F32),
                   jax.ShapeDtypeStruct((bsz, SUBLANES, D_MODEL), F32)),
        grid=(bsz, seq // tm),
        in_specs=[
            pl.BlockSpec((None, tm, D_MODEL), lambda b, i: (b, i, 0)),
            _const_spec((SUBLANES, D_MODEL)),
            _const_spec((1, D_MODEL)),
            _const_spec((D_MODEL, 4 * D_MODEL)),
            _const_spec((CONV_W, D_MODEL)),
            _const_spec((1, D_MODEL)),
            _const_spec((D_MODEL, D_MODEL)),
        ],
        out_specs=(pl.BlockSpec((None, tm, D_MODEL), lambda b, i: (b, i, 0)),
                   pl.BlockSpec((None, SUBLANES, D_MODEL), lambda b, i: (b, 0, 0))),
        scratch_shapes=[pltpu.VMEM((tm, D_MODEL), BF16),
                        pltpu.VMEM((tm, D_MODEL), BF16),
                        pltpu.VMEM((SUBLANES, D_MODEL), F32)],
        compiler_params=_params(2),
        name="conv_layer",
    )(h, halo, g, w_in, conv_w, conv_b, w_out)


def _norm_proj_kernel(h_ref, g_ref, w_ref, a_ref, b_ref, *, a_scale):
    hn = _rmsnorm_rows(h_ref[...], g_ref[...]).astype(BF16)
    a = _dot(hn, w_ref[:, :D_MODEL])
    if a_scale != 1.0:
        a = a * a_scale
    a_ref[...] = a.astype(BF16)
    b_ref[...] = _dot(hn, w_ref[:, D_MODEL:]).astype(BF16)


def _norm_proj(h, g, w, a_scale=1.0):
    bsz, seq, _ = h.shape
    tm = min(seq, ROW_TILE)
    tile = pl.BlockSpec((None, tm, D_MODEL), lambda b, i: (b, i, 0))
    out = jax.ShapeDtypeStruct(h.shape, BF16)
    return pl.pallas_call(
        functools.partial(_norm_proj_kernel, a_scale=a_scale),
        out_shape=(out, out),
        grid=(bsz, seq // tm),
        in_specs=[tile, _const_spec((1, D_MODEL)), _const_spec((D_MODEL, 2 * D_MODEL))],
        out_specs=(tile, tile),
        compiler_params=_params(2),
        name="norm_proj",
    )(h, g, w)


def _out_proj_kernel(h_ref, o_ref, z_ref, w_ref, g_ref, out_ref, *, final_norm):
    y = (o_ref[...].astype(F32) * _silu(z_ref[...].astype(F32))).astype(BF16)
    out = h_ref[...] + _dot(y, w_ref[...])
    if final_norm:
        out = _rmsnorm_rows(out, g_ref[...])
    out_ref[...] = out


def _out_proj(h, o, z, w, g, final_norm):
    bsz, seq, _ = h.shape
    tm = min(seq, ROW_TILE)
    tile = pl.BlockSpec((None, tm, D_MODEL), lambda b, i: (b, i, 0))
    return pl.pallas_call(
        functools.partial(_out_proj_kernel, final_norm=final_norm),
        out_shape=jax.ShapeDtypeStruct(h.shape, F32),
        grid=(bsz, seq // tm),
        in_specs=[tile, tile, tile, _const_spec((D_MODEL, D_MODEL)), _const_spec((1, D_MODEL))],
        out_specs=tile,
        compiler_params=_params(2),
        name="out_proj",
    )(h, o, z, w, g)


def _attn_kernel(qT_ref, k_ref, vT_ref, km_ref, vmT_ref, lq1_ref, lk1_ref, lq2_ref, lk2_ref, g_ref,
                 o_ref, vaug_sc, *, seq, lambda_init):
    tq = Q_TILE
    vaug_sc[0:V_DIM, :] = vT_ref[...]
    vaug_sc[V_DIM:, :] = jnp.ones((BF16_ROWS, seq), BF16)
    lam = (jnp.exp(jnp.sum(lq1_ref[...] * lk1_ref[...], keepdims=True))
           - jnp.exp(jnp.sum(lq2_ref[...] * lk2_ref[...], keepdims=True)) + lambda_init)
    first_map = lax.broadcasted_iota(jnp.int32, (2 * HEAD_DIM, tq), 0) < HEAD_DIM
    kchunk = lax.broadcasted_iota(jnp.int32, (tq, tq), 0) // CHUNK
    qchunk = lax.broadcasted_iota(jnp.int32, (tq, tq), 1) // CHUNK
    diag_mask = kchunk <= qchunk
    km = km_ref[...]
    vm_aug = jnp.concatenate([vmT_ref[...], jnp.ones((BF16_ROWS, N_META), BF16)], axis=0)

    def softmax_pv(qc, n):
        s_meta = _dot(km, qc)
        s = _dot(k_ref[0:n + tq, :], qc)
        s_diag = jnp.where(diag_mask, s[n:], -jnp.inf)
        m = jnp.maximum(jnp.max(s_meta, axis=0, keepdims=True),
                        jnp.max(s_diag, axis=0, keepdims=True))
        if n:
            m = jnp.maximum(m, jnp.max(s[:n], axis=0, keepdims=True))
            s = jnp.concatenate([s[:n], s_diag], axis=0)
        else:
            s = s_diag
        acc = _dot(vm_aug, jnp.exp2(s_meta - m).astype(BF16))
        return acc + _dot(vaug_sc[:, 0:n + tq], jnp.exp2(s - m).astype(BF16))

    for i in range(seq // tq):
        qT = qT_ref[:, i * tq:(i + 1) * tq]
        zero = jnp.zeros_like(qT)
        a1 = softmax_pv(jnp.where(first_map, qT, zero), i * tq)
        a2 = softmax_pv(jnp.where(first_map, zero, qT), i * tq)
        o1 = a1[:V_DIM] * (1.0 / a1[V_DIM:V_DIM + 1])
        o2 = a2[:V_DIM] * (1.0 / a2[V_DIM:V_DIM + 1])
        oT = o1 - lam * o2
        ms = jnp.mean(oT * oT, axis=0, keepdims=True)
        oT = oT * lax.rsqrt(ms + EPS) * g_ref[...] * (1.0 - lambda_init)
        o_ref[i * tq:(i + 1) * tq, :] = oT.T.astype(BF16)


def _attention(qT, k, vT, km, vmT, lq1, lk1, lq2, lk2, g_col, lambda_init):
    bsz, seq, _ = k.shape
    assert seq % Q_TILE == 0 and Q_TILE % CHUNK == 0
    lam_spec = _const_spec((1, HEAD_DIM))
    return pl.pallas_call(
        functools.partial(_attn_kernel, seq=seq, lambda_init=lambda_init),
        out_shape=jax.ShapeDtypeStruct((bsz, seq, D_MODEL), BF16),
        grid=(bsz, N_HEADS),
        in_specs=[
            pl.BlockSpec((None, None, 2 * HEAD_DIM, seq), lambda b, h: (b, h, 0, 0)),
            pl.BlockSpec((None, seq, 2 * HEAD_DIM), lambda b, h: (b, 0, h)),
            pl.BlockSpec((None, None, V_DIM, seq), lambda b, h: (b, h, 0, 0)),
            pl.BlockSpec((N_META, 2 * HEAD_DIM), lambda b, h: (0, h)),
            pl.BlockSpec((None, V_DIM, N_META), lambda b, h: (h, 0, 0)),
            lam_spec, lam_spec, lam_spec, lam_spec,
            _const_spec((V_DIM, 1)),
        ],
        out_specs=pl.BlockSpec((None, seq, V_DIM), lambda b, h: (b, 0, h)),
        scratch_shapes=[pltpu.VMEM((V_DIM + BF16_ROWS, seq), BF16)],
        compiler_params=_params(2),
        name="diff_attention",
    )(qT, k, vT, km, vmT, lq1, lk1, lq2, lk2, g_col)


def kernel(x, meta_tokens, a_norm_g, a_w_in, a_conv_w, a_conv_b, a_w_out, kv_norm_g, w_kv, b_norm_g,
           b_w_in, b_lambda_q1, b_lambda_k1, b_lambda_q2, b_lambda_k2, b_subln_g, b_w_out,
           final_norm_g):
    bsz, seq, _ = x.shape
    n_b = b_w_in.shape[0]
    row = lambda a: a.reshape(1, -1)

    h = x
    hm = meta_tokens[None]
    halo_m = jnp.zeros((SUBLANES, D_MODEL), F32)
    for i in range(N_A):
        args = (row(a_norm_g[i]), a_w_in[i].astype(BF16), a_conv_w[i], row(a_conv_b[i]),
                a_w_out[i].astype(BF16))
        hm, vtail_m = _conv_layer(hm, halo_m, *args)
        h, _ = _conv_layer(h, vtail_m[0], *args)

    w_kv_b = w_kv.astype(BF16)
    k, v = _norm_proj(h, row(kv_norm_g), w_kv_b)
    km, vm = _norm_proj(hm, row(kv_norm_g), w_kv_b)
    vT = v.reshape(bsz, seq, N_HEADS, V_DIM).transpose(0, 2, 3, 1)
    vmT = vm.reshape(N_META, N_HEADS, V_DIM).transpose(1, 2, 0)
    km = km[0]

    for j in range(n_b):
        lambda_init = _lambda_init(N_A + j)
        q, z = _norm_proj(h, row(b_norm_g[j]), b_w_in[j].astype(BF16), a_scale=Q_SCALE)
        qT = q.reshape(bsz, seq, N_HEADS, 2 * HEAD_DIM).transpose(0, 2, 3, 1)
        o = _attention(qT, k, vT, km, vmT, row(b_lambda_q1[j]), row(b_lambda_k1[j]),
                       row(b_lambda_q2[j]), row(b_lambda_k2[j]), b_subln_g[j].reshape(V_DIM, 1),
                       lambda_init)
        h = _out_proj(h, o, z, b_w_out[j].astype(BF16), row(final_norm_g),
                      final_norm=(j == n_b - 1))
    return h
```

```python
import functools
import math

import jax
import jax.numpy as jnp
from jax import lax
from jax.experimental import pallas as pl
from jax.experimental.pallas import tpu as pltpu

D_MODEL = 1024
N_HEADS = 8
HEAD_DIM = 64
V_DIM = 128
N_META = 16
CHUNK = 64
CONV_W = 3
N_A = 2
EPS = 1e-6

F32 = jnp.float32
BF16 = jnp.bfloat16

V7X_VMEM_BYTES = 64 * 1024 * 1024
VMEM_LIMIT_BYTES = V7X_VMEM_BYTES * 7 // 8
SUBLANES = 8
BF16_ROWS = 16

ROW_TILE = 512
CONV_COLS = 256
Q_TILE = 256
Q_SCALE = HEAD_DIM ** -0.5 * math.log2(math.e)


def _lambda_init(depth):
    return 0.8 - 0.6 * math.exp(-0.3 * depth)


def _rmsnorm_rows(x, g):
    ms = jnp.mean(x * x, axis=-1, keepdims=True)
    return x * lax.rsqrt(ms + EPS) * g


def _silu(z):
    return z / (1.0 + jnp.exp(-z))


def _dot(a, b):
    return jnp.dot(a, b, preferred_element_type=F32)


def _params(n_grid):
    return pltpu.CompilerParams(
        dimension_semantics=("arbitrary",) * n_grid,
        vmem_limit_bytes=VMEM_LIMIT_BYTES,
    )


def _const_spec(shape):
    return pl.BlockSpec(shape, lambda *_: (0,) * len(shape))


def _conv_layer_kernel(h_ref, halo_ref, g_ref, win_ref, cw_ref, cb_ref, wout_ref,
                       out_ref, vtail_ref, hn_sc, y_sc, carry_sc, *, tm):
    @pl.when(pl.program_id(1) == 0)
    def _():
        carry_sc[...] = halo_ref[...]

    x = h_ref[...]
    hn_sc[...] = _rmsnorm_rows(x, g_ref[...]).astype(BF16)
    fix = min(tm, BF16_ROWS)
    for j in range(D_MODEL // CONV_COLS):
        lo = j * CONV_COLS
        cols = slice(lo, lo + CONV_COLS)
        hn = hn_sc[...]

        def proj(part):
            return _dot(hn, win_ref[:, part * D_MODEL + lo:part * D_MODEL + lo + CONV_COLS])

        v = proj(1) * proj(2)
        prev = carry_sc[:, cols]
        carry_sc[:, cols] = v[tm - SUBLANES:, :]
        w0 = cw_ref[0:1, cols]
        w1 = cw_ref[1:2, cols]
        w2 = cw_ref[2:3, cols]
        cb = cb_ref[:, cols]
        conv = w0 * pltpu.roll(v, 2, axis=0) + w1 * pltpu.roll(v, 1, axis=0) + w2 * v + cb
        z = proj(3)
        gate = proj(0) * _silu(z)
        y_sc[:, cols] = (gate * conv).astype(BF16)
        ext = jnp.concatenate([prev, v[:fix]], axis=0)
        e1 = pltpu.roll(ext, 1, axis=0)[SUBLANES:]
        e2 = pltpu.roll(ext, 2, axis=0)[SUBLANES:]
        conv_head = w0 * e2 + w1 * e1 + w2 * v[:fix] + cb
        y_sc[0:fix, cols] = (gate[:fix] * conv_head).astype(BF16)
    vtail_ref[...] = carry_sc[...]
    out_ref[...] = x + _dot(y_sc[...], wout_ref[...])


def _conv_layer(h, halo, g, w_in, conv_w, conv_b, w_out):
    bsz, seq, _ = h.shape
    tm = min(seq, ROW_TILE)
    kern = functools.partial(_conv_layer_kernel, tm=tm)
    return pl.pallas_call(
        kern,
        out_shape=(jax.ShapeDtypeStruct(h.shape, F32),
                   jax.ShapeDtypeStruct((bsz, SUBLANES, D_MODEL), F32)),
        grid=(bsz, seq // tm),
        in_specs=[
            pl.BlockSpec((None, tm, D_MODEL), lambda b, i: (b, i, 0)),
            _const_spec((SUBLANES, D_MODEL)),
            _const_spec((1, D_MODEL)),
            _const_spec((D_MODEL, 4 * D_MODEL)),
            _const_spec((CONV_W, D_MODEL)),
            _const_spec((1, D_MODEL)),
            _const_spec((D_MODEL, D_MODEL)),
        ],
        out_specs=(pl.BlockSpec((None, tm, D_MODEL), lambda b, i: (b, i, 0)),
                   pl.BlockSpec((None, SUBLANES, D_MODEL), lambda b, i: (b, 0, 0))),
        scratch_shapes=[pltpu.VMEM((tm, D_MODEL), BF16),
                        pltpu.VMEM((tm, D_MODEL), BF16),
                        pltpu.VMEM((SUBLANES, D_MODEL), F32)],
        compiler_params=_params(2),
        name="conv_layer",
    )(h, halo, g, w_in, conv_w, conv_b, w_out)


def _norm_proj_kernel(h_ref, g_ref, w_ref, a_ref, b_ref, *, a_scale):
    hn = _rmsnorm_rows(h_ref[...], g_ref[...]).astype(BF16)
    a = _dot(hn, w_ref[:, :D_MODEL])
    if a_scale != 1.0:
        a = a * a_scale
    a_ref[...] = a.astype(BF16)
    b_ref[...] = _dot(hn, w_ref[:, D_MODEL:]).astype(BF16)


def _norm_proj(h, g, w, a_scale=1.0):
    bsz, seq, _ = h.shape
    tm = min(seq, ROW_TILE)
    tile = pl.BlockSpec((None, tm, D_MODEL), lambda b, i: (b, i, 0))
    out = jax.ShapeDtypeStruct(h.shape, BF16)
    return pl.pallas_call(
        functools.partial(_norm_proj_kernel, a_scale=a_scale),
        out_shape=(out, out),
        grid=(bsz, seq // tm),
        in_specs=[tile, _const_spec((1, D_MODEL)), _const_spec((D_MODEL, 2 * D_MODEL))],
        out_specs=(tile, tile),
        compiler_params=_params(2),
        name="norm_proj",
    )(h, g, w)


def _out_proj_kernel(h_ref, o_ref, z_ref, w_ref, g_ref, out_ref, *, final_norm):
    y = (o_ref[...].astype(F32) * _silu(z_ref[...].astype(F32))).astype(BF16)
    out = h_ref[...] + _dot(y, w_ref[...])
    if final_norm:
        out = _rmsnorm_rows(out, g_ref[...])
    out_ref[...] = out


def _out_proj(h, o, z, w, g, final_norm):
    bsz, seq, _ = h.shape
    tm = min(seq, ROW_TILE)
    tile = pl.BlockSpec((None, tm, D_MODEL), lambda b, i: (b, i, 0))
    return pl.pallas_call(
        functools.partial(_out_proj_kernel, final_norm=final_norm),
        out_shape=jax.ShapeDtypeStruct(h.shape, F32),
        grid=(bsz, seq // tm),
        in_specs=[tile, tile, tile, _const_spec((D_MODEL, D_MODEL)), _const_spec((1, D_MODEL))],
        out_specs=tile,
        compiler_params=_params(2),
        name="out_proj",
    )(h, o, z, w, g)


def _attn_kernel(qT_ref, k_ref, vT_ref, km_ref, vmT_ref, lq1_ref, lk1_ref, lq2_ref, lk2_ref, g_ref,
                 o_ref, vaug_sc, *, seq, lambda_init):
    tq = Q_TILE
    vaug_sc[0:V_DIM, :] = vT_ref[...]
    vaug_sc[V_DIM:, :] = jnp.ones((BF16_ROWS, seq), BF16)
    lam = (jnp.exp(jnp.sum(lq1_ref[...] * lk1_ref[...], keepdims=True))
           - jnp.exp(jnp.sum(lq2_ref[...] * lk2_ref[...], keepdims=True)) + lambda_init)
    first_map = lax.broadcasted_iota(jnp.int32, (2 * HEAD_DIM, tq), 0) < HEAD_DIM
    kchunk = lax.broadcasted_iota(jnp.int32, (tq, 2 * tq), 0) // CHUNK
    qchunk = (lax.broadcasted_iota(jnp.int32, (tq, 2 * tq), 1) % tq) // CHUNK
    diag_mask = kchunk <= qchunk
    km = km_ref[...]
    vm_aug = jnp.concatenate([vmT_ref[...], jnp.ones((BF16_ROWS, N_META), BF16)], axis=0)

    for i in range(seq // tq):
        n = i * tq
        qT = qT_ref[:, n:n + tq]
        zero = jnp.zeros_like(qT)
        qc = jnp.concatenate([jnp.where(first_map, qT, zero), jnp.where(first_map, zero, qT)], axis=1)
        s_meta = _dot(km, qc)
        s = _dot(k_ref[0:n + tq, :], qc)
        s_diag = jnp.where(diag_mask, s[n:], -jnp.inf)
        m = jnp.maximum(jnp.max(s_meta, axis=0, keepdims=True),
                        jnp.max(s_diag, axis=0, keepdims=True))
        if n:
            m = jnp.maximum(m, jnp.max(s[:n], axis=0, keepdims=True))
            s = jnp.concatenate([s[:n], s_diag], axis=0)
        else:
            s = s_diag
        acc = (_dot(vm_aug, jnp.exp2(s_meta - m).astype(BF16))
               + _dot(vaug_sc[:, 0:n + tq], jnp.exp2(s - m).astype(BF16)))
        o = acc[:V_DIM] * (1.0 / acc[V_DIM:V_DIM + 1])
        o1 = o[:, :tq]
        o2 = o[:, tq:]
        oT = o1 - lam * o2
        ms = jnp.mean(oT * oT, axis=0, keepdims=True)
        oT = oT * lax.rsqrt(ms + EPS) * g_ref[...] * (1.0 - lambda_init)
        o_ref[i * tq:(i + 1) * tq, :] = oT.T.astype(BF16)


def _attention(qT, k, vT, km, vmT, lq1, lk1, lq2, lk2, g_col, lambda_init):
    bsz, seq, _ = k.shape
    assert seq % Q_TILE == 0 and Q_TILE % CHUNK == 0
    lam_spec = _const_spec((1, HEAD_DIM))
    return pl.pallas_call(
        functools.partial(_attn_kernel, seq=seq, lambda_init=lambda_init),
        out_shape=jax.ShapeDtypeStruct((bsz, seq, D_MODEL), BF16),
        grid=(bsz, N_HEADS),
        in_specs=[
            pl.BlockSpec((None, None, 2 * HEAD_DIM, seq), lambda b, h: (b, h, 0, 0)),
            pl.BlockSpec((None, seq, 2 * HEAD_DIM), lambda b, h: (b, 0, h)),
            pl.BlockSpec((None, None, V_DIM, seq), lambda b, h: (b, h, 0, 0)),
            pl.BlockSpec((N_META, 2 * HEAD_DIM), lambda b, h: (0, h)),
            pl.BlockSpec((None, V_DIM, N_META), lambda b, h: (h, 0, 0)),
            lam_spec, lam_spec, lam_spec, lam_spec,
            _const_spec((V_DIM, 1)),
        ],
        out_specs=pl.BlockSpec((None, seq, V_DIM), lambda b, h: (b, 0, h)),
        scratch_shapes=[pltpu.VMEM((V_DIM + BF16_ROWS, seq), BF16)],
        compiler_params=_params(2),
        name="diff_attention",
    )(qT, k, vT, km, vmT, lq1, lk1, lq2, lk2, g_col)


def kernel(x, meta_tokens, a_norm_g, a_w_in, a_conv_w, a_conv_b, a_w_out, kv_norm_g, w_kv, b_norm_g,
           b_w_in, b_lambda_q1, b_lambda_k1, b_lambda_q2, b_lambda_k2, b_subln_g, b_w_out,
           final_norm_g):
    bsz, seq, _ = x.shape
    n_b = b_w_in.shape[0]
    row = lambda a: a.reshape(1, -1)

    h = x
    hm = meta_tokens[None]
    halo_m = jnp.zeros((SUBLANES, D_MODEL), F32)
    for i in range(N_A):
        args = (row(a_norm_g[i]), a_w_in[i].astype(BF16), a_conv_w[i], row(a_conv_b[i]),
                a_w_out[i].astype(BF16))
        hm, vtail_m = _conv_layer(hm, halo_m, *args)
        h, _ = _conv_layer(h, vtail_m[0], *args)

    w_kv_b = w_kv.astype(BF16)
    k, v = _norm_proj(h, row(kv_norm_g), w_kv_b)
    km, vm = _norm_proj(hm, row(kv_norm_g), w_kv_b)
    vT = v.reshape(bsz, seq, N_HEADS, V_DIM).transpose(0, 2, 3, 1)
    vmT = vm.reshape(N_META, N_HEADS, V_DIM).transpose(1, 2, 0)
    km = km[0]

    for j in range(n_b):
        lambda_init = _lambda_init(N_A + j)
        q, z = _norm_proj(h, row(b_norm_g[j]), b_w_in[j].astype(BF16), a_scale=Q_SCALE)
        qT = q.reshape(bsz, seq, N_HEADS, 2 * HEAD_DIM).transpose(0, 2, 3, 1)
        o = _attention(qT, k, vT, km, vmT, row(b_lambda_q1[j]), row(b_lambda_k1[j]),
                       row(b_lambda_q2[j]), row(b_lambda_k2[j]), b_subln_g[j].reshape(V_DIM, 1),
                       lambda_init)
        h = _out_proj(h, o, z, b_w_out[j].astype(BF16), row(final_norm_g),
                      final_norm=(j == n_b - 1))
    return h
```

```python
import functools
import math

import jax
import jax.numpy as jnp
from jax import lax
from jax.experimental import pallas as pl
from jax.experimental.pallas import tpu as pltpu

D_MODEL = 1024
N_HEADS = 8
HEAD_DIM = 64
V_DIM = 128
N_META = 16
CHUNK = 64
CONV_W = 3
N_A = 2
EPS = 1e-6

F32 = jnp.float32
BF16 = jnp.bfloat16

V7X_VMEM_BYTES = 64 * 1024 * 1024
VMEM_LIMIT_BYTES = V7X_VMEM_BYTES * 7 // 8
SUBLANES = 8
BF16_ROWS = 16

ROW_TILE = 512
CONV_COLS = 256
Q_TILE = 256
Q_SCALE = HEAD_DIM ** -0.5 * math.log2(math.e)


def _lambda_init(depth):
    return 0.8 - 0.6 * math.exp(-0.3 * depth)


def _rmsnorm_rows(x, g):
    ms = jnp.mean(x * x, axis=-1, keepdims=True)
    return x * lax.rsqrt(ms + EPS) * g


def _silu(z):
    return z / (1.0 + jnp.exp(-z))


def _dot(a, b):
    return jnp.dot(a, b, preferred_element_type=F32)


def _params(n_grid):
    return pltpu.CompilerParams(
        dimension_semantics=("arbitrary",) * n_grid,
        vmem_limit_bytes=VMEM_LIMIT_BYTES,
    )


def _const_spec(shape):
    return pl.BlockSpec(shape, lambda *_: (0,) * len(shape))


def _conv_layer_kernel(h_ref, halo_ref, g_ref, win_ref, cw_ref, cb_ref, wout_ref,
                       out_ref, vtail_ref, hn_sc, y_sc, carry_sc, *, tm):
    @pl.when(pl.program_id(1) == 0)
    def _():
        carry_sc[...] = halo_ref[...]

    x = h_ref[...]
    hn_sc[...] = _rmsnorm_rows(x, g_ref[...]).astype(BF16)
    fix = min(tm, BF16_ROWS)
    for j in range(D_MODEL // CONV_COLS):
        lo = j * CONV_COLS
        cols = slice(lo, lo + CONV_COLS)
        hn = hn_sc[...]

        def proj(part):
            return _dot(hn, win_ref[:, part * D_MODEL + lo:part * D_MODEL + lo + CONV_COLS])

        v = proj(1) * proj(2)
        prev = carry_sc[:, cols]
        carry_sc[:, cols] = v[tm - SUBLANES:, :]
        w0 = cw_ref[0:1, cols]
        w1 = cw_ref[1:2, cols]
        w2 = cw_ref[2:3, cols]
        cb = cb_ref[:, cols]
        conv = w0 * pltpu.roll(v, 2, axis=0) + w1 * pltpu.roll(v, 1, axis=0) + w2 * v + cb
        z = proj(3)
        gate = proj(0) * _silu(z)
        y_sc[:, cols] = (gate * conv).astype(BF16)
        ext = jnp.concatenate([prev, v[:fix]], axis=0)
        e1 = pltpu.roll(ext, 1, axis=0)[SUBLANES:]
        e2 = pltpu.roll(ext, 2, axis=0)[SUBLANES:]
        conv_head = w0 * e2 + w1 * e1 + w2 * v[:fix] + cb
        y_sc[0:fix, cols] = (gate[:fix] * conv_head).astype(BF16)
    vtail_ref[...] = carry_sc[...]
    out_ref[...] = x + _dot(y_sc[...], wout_ref[...])


def _conv_layer(h, halo, g, w_in, conv_w, conv_b, w_out):
    bsz, seq, _ = h.shape
    tm = min(seq, ROW_TILE)
    kern = functools.partial(_conv_layer_kernel, tm=tm)
    return pl.pallas_call(
        kern,
        out_shape=(jax.ShapeDtypeStruct(h.shape, F32),
                   jax.ShapeDtypeStruct((bsz, SUBLANES, D_MODEL), F32)),
        grid=(bsz, seq // tm),
        in_specs=[
            pl.BlockSpec((None, tm, D_MODEL), lambda b, i: (b, i, 0)),
            _const_spec((SUBLANES, D_MODEL)),
            _const_spec((1, D_MODEL)),
            _const_spec((D_MODEL, 4 * D_MODEL)),
            _const_spec((CONV_W, D_MODEL)),
            _const_spec((1, D_MODEL)),
            _const_spec((D_MODEL, D_MODEL)),
        ],
        out_specs=(pl.BlockSpec((None, tm, D_MODEL), lambda b, i: (b, i, 0)),
                   pl.BlockSpec((None, SUBLANES, D_MODEL), lambda b, i: (b, 0, 0))),
        scratch_shapes=[pltpu.VMEM((tm, D_MODEL), BF16),
                        pltpu.VMEM((tm, D_MODEL), BF16),
                        pltpu.VMEM((SUBLANES, D_MODEL), F32)],
        compiler_params=_params(2),
        name="conv_layer",
    )(h, halo, g, w_in, conv_w, conv_b, w_out)


def _norm_proj_kernel(h_ref, g_ref, w_ref, a_ref, b_ref, *, a_scale):
    hn = _rmsnorm_rows(h_ref[...], g_ref[...]).astype(BF16)
    a = _dot(hn, w_ref[:, :D_MODEL])
    if a_scale != 1.0:
        a = a * a_scale
    a_ref[...] = a.astype(BF16)
    b_ref[...] = _dot(hn, w_ref[:, D_MODEL:]).astype(BF16)


def _norm_proj(h, g, w, a_scale=1.0):
    bsz, seq, _ = h.shape
    tm = min(seq, ROW_TILE)
    tile = pl.BlockSpec((None, tm, D_MODEL), lambda b, i: (b, i, 0))
    out = jax.ShapeDtypeStruct(h.shape, BF16)
    return pl.pallas_call(
        functools.partial(_norm_proj_kernel, a_scale=a_scale),
        out_shape=(out, out),
        grid=(bsz, seq // tm),
        in_specs=[tile, _const_spec((1, D_MODEL)), _const_spec((D_MODEL, 2 * D_MODEL))],
        out_specs=(tile, tile),
        compiler_params=_params(2),
        name="norm_proj",
    )(h, g, w)


def _out_proj_kernel(h_ref, o_ref, z_ref, w_ref, g_ref, out_ref, *, final_norm):
    y = (o_ref[...].astype(F32) * _silu(z_ref[...].astype(F32))).astype(BF16)
    out = h_ref[...] + _dot(y, w_ref[...])
    if final_norm:
        out = _rmsnorm_rows(out, g_ref[...])
    out_ref[...] = out


def _out_proj(h, o, z, w, g, final_norm):
    bsz, seq, _ = h.shape
    tm = min(seq, ROW_TILE)
    tile = pl.BlockSpec((None, tm, D_MODEL), lambda b, i: (b, i, 0))
    return pl.pallas_call(
        functools.partial(_out_proj_kernel, final_norm=final_norm),
        out_shape=jax.ShapeDtypeStruct(h.shape, F32),
        grid=(bsz, seq // tm),
        in_specs=[tile, tile, tile, _const_spec((D_MODEL, D_MODEL)), _const_spec((1, D_MODEL))],
        out_specs=tile,
        compiler_params=_params(2),
        name="out_proj",
    )(h, o, z, w, g)


def _attn_kernel(qT_ref, k_ref, vT_ref, km_ref, vmT_ref, lq1_ref, lk1_ref, lq2_ref, lk2_ref, g_ref,
                 o_ref, vaug_sc, s_sc, p_sc, *, seq, lambda_init):
    tq = Q_TILE
    vaug_sc[0:V_DIM, :] = vT_ref[...]
    vaug_sc[V_DIM:, :] = jnp.ones((BF16_ROWS, seq), BF16)
    lam = (jnp.exp(jnp.sum(lq1_ref[...] * lk1_ref[...], keepdims=True))
           - jnp.exp(jnp.sum(lq2_ref[...] * lk2_ref[...], keepdims=True)) + lambda_init)
    first_map = lax.broadcasted_iota(jnp.int32, (2 * HEAD_DIM, tq), 0) < HEAD_DIM
    kchunk = lax.broadcasted_iota(jnp.int32, (tq, 2 * tq), 0) // CHUNK
    qchunk = (lax.broadcasted_iota(jnp.int32, (tq, 2 * tq), 1) % tq) // CHUNK
    diag_mask = kchunk <= qchunk
    km = km_ref[...]
    vm_aug = jnp.concatenate([vmT_ref[...], jnp.ones((BF16_ROWS, N_META), BF16)], axis=0)

    def query_cols(i):
        qT = qT_ref[:, i * tq:(i + 1) * tq]
        zero = jnp.zeros_like(qT)
        return jnp.concatenate([jnp.where(first_map, qT, zero), jnp.where(first_map, zero, qT)], axis=1)

    def score_chunk(i, c, qc, m8):
        rows = slice(c * tq, (c + 1) * tq)
        s = _dot(k_ref[rows, :], qc)
        if c == i:
            s = jnp.where(diag_mask, s, -jnp.inf)
        s_sc[i % 2, rows, :] = s
        return jnp.maximum(m8, jnp.max(s.reshape(tq // SUBLANES, SUBLANES, 2 * tq), axis=0))

    def exp_chunk(i, c, m):
        rows = slice(c * tq, (c + 1) * tq)
        p_sc[i % 2, rows, :] = jnp.exp2(s_sc[i % 2, rows, :] - m).astype(BF16)

    def start_tile(i):
        qc = query_cols(i)
        s_meta = _dot(km, qc)
        m8 = jnp.max(s_meta.reshape(N_META // SUBLANES, SUBLANES, 2 * tq), axis=0)
        return qc, s_meta, m8

    n_tiles = seq // tq
    qc, s_meta, m8 = start_tile(0)
    m8 = score_chunk(0, 0, qc, m8)
    for i in range(n_tiles):
        m = jnp.max(m8, axis=0, keepdims=True)
        p_meta = jnp.exp2(s_meta - m).astype(BF16)
        if i + 1 < n_tiles:
            qc, s_meta, m8 = start_tile(i + 1)
        for c in range(i + 2):
            if i + 1 < n_tiles:
                m8 = score_chunk(i + 1, c, qc, m8)
            if c <= i:
                exp_chunk(i, c, m)
        acc = (_dot(vm_aug, p_meta)
               + _dot(vaug_sc[:, 0:(i + 1) * tq], p_sc[i % 2, 0:(i + 1) * tq, :]))
        o = acc[:V_DIM] * (1.0 / acc[V_DIM:V_DIM + 1])
        o1 = o[:, :tq]
        o2 = o[:, tq:]
        oT = o1 - lam * o2
        ms = jnp.mean(oT * oT, axis=0, keepdims=True)
        oT = oT * lax.rsqrt(ms + EPS) * g_ref[...] * (1.0 - lambda_init)
        o_ref[i * tq:(i + 1) * tq, :] = oT.T.astype(BF16)


def _attention(qT, k, vT, km, vmT, lq1, lk1, lq2, lk2, g_col, lambda_init):
    bsz, seq, _ = k.shape
    assert seq % Q_TILE == 0 and Q_TILE % CHUNK == 0
    lam_spec = _const_spec((1, HEAD_DIM))
    return pl.pallas_call(
        functools.partial(_attn_kernel, seq=seq, lambda_init=lambda_init),
        out_shape=jax.ShapeDtypeStruct((bsz, seq, D_MODEL), BF16),
        grid=(bsz, N_HEADS),
        in_specs=[
            pl.BlockSpec((None, None, 2 * HEAD_DIM, seq), lambda b, h: (b, h, 0, 0)),
            pl.BlockSpec((None, seq, 2 * HEAD_DIM), lambda b, h: (b, 0, h)),
            pl.BlockSpec((None, None, V_DIM, seq), lambda b, h: (b, h, 0, 0)),
            pl.BlockSpec((N_META, 2 * HEAD_DIM), lambda b, h: (0, h)),
            pl.BlockSpec((None, V_DIM, N_META), lambda b, h: (h, 0, 0)),
            lam_spec, lam_spec, lam_spec, lam_spec,
            _const_spec((V_DIM, 1)),
        ],
        out_specs=pl.BlockSpec((None, seq, V_DIM), lambda b, h: (b, 0, h)),
        scratch_shapes=[pltpu.VMEM((V_DIM + BF16_ROWS, seq), BF16),
                        pltpu.VMEM((2, seq, 2 * Q_TILE), F32),
                        pltpu.VMEM((2, seq, 2 * Q_TILE), BF16)],
        compiler_params=_params(2),
        name="diff_attention",
    )(qT, k, vT, km, vmT, lq1, lk1, lq2, lk2, g_col)


def kernel(x, meta_tokens, a_norm_g, a_w_in, a_conv_w, a_conv_b, a_w_out, kv_norm_g, w_kv, b_norm_g,
           b_w_in, b_lambda_q1, b_lambda_k1, b_lambda_q2, b_lambda_k2, b_subln_g, b_w_out,
           final_norm_g):
    bsz, seq, _ = x.shape
    n_b = b_w_in.shape[0]
    row = lambda a: a.reshape(1, -1)

    h = x
    hm = meta_tokens[None]
    halo_m = jnp.zeros((SUBLANES, D_MODEL), F32)
    for i in range(N_A):
        args = (row(a_norm_g[i]), a_w_in[i].astype(BF16), a_conv_w[i], row(a_conv_b[i]),
                a_w_out[i].astype(BF16))
        hm, vtail_m = _conv_layer(hm, halo_m, *args)
        h, _ = _conv_layer(h, vtail_m[0], *args)

    w_kv_b = w_kv.astype(BF16)
    k, v = _norm_proj(h, row(kv_norm_g), w_kv_b)
    km, vm = _norm_proj(hm, row(kv_norm_g), w_kv_b)
    vT = v.reshape(bsz, seq, N_HEADS, V_DIM).transpose(0, 2, 3, 1)
    vmT = vm.reshape(N_META, N_HEADS, V_DIM).transpose(1, 2, 0)
    km = km[0]

    for j in range(n_b):
        lambda_init = _lambda_init(N_A + j)
        q, z = _norm_proj(h, row(b_norm_g[j]), b_w_in[j].astype(BF16), a_scale=Q_SCALE)
        qT = q.reshape(bsz, seq, N_HEADS, 2 * HEAD_DIM).transpose(0, 2, 3, 1)
        o = _attention(qT, k, vT, km, vmT, row(b_lambda_q1[j]), row(b_lambda_k1[j]),
                       row(b_lambda_q2[j]), row(b_lambda_k2[j]), b_subln_g[j].reshape(V_DIM, 1),
                       lambda_init)
        h = _out_proj(h, o, z, b_w_out[j].astype(BF16), row(final_norm_g),
                      final_norm=(j == n_b - 1))
    return h
```
